```python
import math, functools
import jax, jax.numpy as jnp
from jax import lax
import numpy as np


D_MODEL = 2048
BATCH = 1
SEQ = 8192
DEPTH = 1
DEC_BATCH = 128
DEC_SEQ = 1
PAST_LEN = 2048
PAGE_SIZE = 128

HEAD_DIM = 128
MIX_WIDTH = D_MODEL
N_ATTN_HEADS = 8
N_KV_HEADS = 2
GROUP = N_ATTN_HEADS // N_KV_HEADS
CONV_CH = MIX_WIDTH - N_ATTN_HEADS * HEAD_DIM
CONV_WIDTH = 31
ROPE_DIM = HEAD_DIM // 4
ROPE_THETA = 500000.0
CMP_BLOCK = 32
CMP_STRIDE = 16
CMP_HIDDEN = 256
SEL_BLOCK = 64
N_SEL = 16
WINDOW = 512
Q_BLOCK = 128
FFN_DIM = 5632
FFN_CONV_WIDTH = 3
PLE_DIM = 256
EPS = 1e-6
SCALE = HEAD_DIM ** -0.5
NEG = -1e30
FORCED = 1e9
Q_COLS = N_ATTN_HEADS * HEAD_DIM
KV_COLS = 6 * N_KV_HEADS * HEAD_DIM
GATE_COLS = 3 * N_ATTN_HEADS
W_IN_COLS = 2 * CONV_CH + Q_COLS + KV_COLS + GATE_COLS

kernel_name = "hymba_conformer_nsa_step"


def rmsnorm(x, g):
    xf = x.astype(jnp.float32)
    y = xf * lax.rsqrt(jnp.mean(xf * xf, axis=-1, keepdims=True) + EPS)
    return (y * g.astype(jnp.float32)).astype(x.dtype)


def layernorm(x, g, b):
    xf = x.astype(jnp.float32)
    xc = xf - jnp.mean(xf, axis=-1, keepdims=True)
    y = xc * lax.rsqrt(jnp.mean(xc * xc, axis=-1, keepdims=True) + EPS)
    return (y * g.astype(jnp.float32) + b.astype(jnp.float32)).astype(x.dtype)


def rope(x, pos):
    half = ROPE_DIM // 2
    inv = ROPE_THETA ** (-jnp.arange(half, dtype=jnp.float32) / half)
    ang = pos.astype(jnp.float32)[:, None] * inv[None, :]
    cos = jnp.cos(ang)[None, :, None, :]
    sin = jnp.sin(ang)[None, :, None, :]
    xr = x[..., :ROPE_DIM].astype(jnp.float32)
    x1, x2 = xr[..., :half], xr[..., half:]
    rot = jnp.concatenate([x1 * cos - x2 * sin, x2 * cos + x1 * sin], axis=-1)
    return jnp.concatenate([rot.astype(x.dtype), x[..., ROPE_DIM:]], axis=-1)


def causal_dwconv(u, buf, w, b):
    width = w.shape[0]
    full = jnp.concatenate([buf, u], axis=1)
    y = lax.conv_general_dilated(full, w[:, None, :], window_strides=(1,), padding='VALID',
                                 dimension_numbers=('NWC', 'WIO', 'NWC'),
                                 feature_group_count=u.shape[-1])
    return y + b, full[:, full.shape[1] - (width - 1):]


def masked_softmax(s, mask):
    s = jnp.where(mask, s, NEG)
    m = jnp.max(s, axis=-1, keepdims=True)
    e = jnp.where(mask, jnp.exp(s - m), 0.0)
    return e / jnp.maximum(jnp.sum(e, axis=-1, keepdims=True), 1e-30)


def attend(q, k, v, mask):
    s = jnp.einsum('...qhgd,...khd->...hgqk', q, k).astype(jnp.float32) * SCALE
    p = masked_softmax(s, mask[..., None, None, :, :])
    o = jnp.einsum('...hgqk,...khd->...qhgd', p.astype(v.dtype), v)
    return o, p


def window_mask(qp, kp):
    return (kp <= qp) & (qp - kp < WINDOW) & (kp >= 0)


def mixer_inputs(n, w_in, pos):
    B, T, _ = n.shape
    proj = n @ w_in
    o1 = 2 * CONV_CH
    o2 = o1 + Q_COLS
    o3 = o2 + KV_COLS
    u = proj[..., :CONV_CH] * jax.nn.sigmoid(proj[..., CONV_CH:o1])
    q = rope(proj[..., o1:o2].reshape(B, T, N_ATTN_HEADS, HEAD_DIM), pos)
    kv = proj[..., o2:o3].reshape(B, T, 3, 2, N_KV_HEADS, HEAD_DIM)
    k = rope(kv[:, :, :, 0].reshape(B, T, 3 * N_KV_HEADS, HEAD_DIM), pos)
    k = k.reshape(B, T, 3, N_KV_HEADS, HEAD_DIM)
    kv = jnp.stack([k, kv[:, :, :, 1]], axis=4)
    gates = jax.nn.sigmoid(proj[..., o3:]).reshape(B, T, 3, N_KV_HEADS, GROUP)
    return u, q, kv[:, :, 0], kv[:, :, 1], kv[:, :, 2], gates


def compress(kv, phi_pos, phi_w1, phi_w2):
    B, T = kv.shape[:2]
    x = kv.transpose(3, 0, 2, 1, 4).reshape(2, B * N_KV_HEADS, T, HEAD_DIM)

    def phi(xc, pos_c, w1c, w2c):
        h = lax.conv_general_dilated(xc, w1c, window_strides=(CMP_STRIDE,), padding='VALID',
                                     dimension_numbers=('NWC', 'WIO', 'NWC'))
        h = h + jnp.einsum('ld,ldh->h', pos_c, w1c)
        return jax.nn.gelu(h) @ w2c

    c = jax.vmap(phi)(x, phi_pos, phi_w1, phi_w2)
    n_cmp = c.shape[2]
    return c.reshape(2, B, N_KV_HEADS, n_cmp, HEAD_DIM).transpose(1, 3, 2, 0, 4)


def cmp_branch(q, q_pos, kv_full, phi_pos, phi_w1, phi_w2):
    B, Tq = q.shape[:2]
    kvc = compress(kv_full, phi_pos, phi_w1, phi_w2)
    n_cmp = kvc.shape[1]
    ends = jnp.arange(n_cmp) * CMP_STRIDE + (CMP_BLOCK - 1)
    mask = ends[None, :] <= q_pos[:, None]
    qg = q.reshape(B, Tq, N_KV_HEADS, GROUP, HEAD_DIM)
    o, p = attend(qg, kvc[..., 0, :], kvc[..., 1, :], mask)
    return o, jnp.sum(p, axis=2)


def select_blocks(imp, q_pos, n_sel):
    n_cmp = imp.shape[-1]
    i = jnp.arange(n_cmp)[:, None] * CMP_STRIDE
    j = jnp.arange(n_sel)[None, :] * SEL_BLOCK
    overlap = ((i < j + SEL_BLOCK) & (i + CMP_BLOCK > j)).astype(jnp.float32)
    score = jnp.einsum('bhqc,cs->bhqs', imp, overlap)
    cur = (q_pos // SEL_BLOCK)[:, None]
    jj = jnp.arange(n_sel)[None, :]
    valid = jj <= cur
    forced = (jj == 0) | (jj == cur) | (jj == cur - 1)
    score = jnp.where(valid, jnp.where(forced, FORCED, score), NEG)
    top, idx = lax.top_k(score, min(N_SEL, n_sel))
    return idx, top > NEG / 2


def rows_from_seq(kv_t, pos):
    B, _, T = kv_t.shape[:3]
    b = jnp.arange(B)[:, None, None, None]
    h = jnp.arange(N_KV_HEADS)[None, :, None, None]
    return kv_t[b, h, jnp.clip(pos, 0, T - 1)]


def rows_from_pages(cache, page_table, new_kv, pos):
    DB, n_pages = page_table.shape
    past_len = n_pages * PAGE_SIZE
    b = jnp.arange(DB)[:, None, None, None]
    h = jnp.arange(N_KV_HEADS)[None, :, None, None]
    phys = page_table[b, jnp.clip(pos // PAGE_SIZE, 0, n_pages - 1)]
    past = cache[phys, pos % PAGE_SIZE, h]
    new = new_kv[b, jnp.clip(pos - past_len, 0, new_kv.shape[1] - 1), h]
    return jnp.where((pos < past_len)[..., None, None], past, new)


def slc_attend(q, q_pos, idx, ok, row_fn):
    B, Tq = q.shape[:2]
    n_top = idx.shape[-1]
    qc = math.gcd(Tq, Q_BLOCK)
    nc = Tq // qc
    qs = q.reshape(B, nc, qc, N_KV_HEADS, GROUP, HEAD_DIM).transpose(1, 0, 2, 3, 4, 5)
    ids = idx.reshape(B, N_KV_HEADS, nc, qc, n_top).transpose(2, 0, 1, 3, 4)
    oks = ok.reshape(B, N_KV_HEADS, nc, qc, n_top).transpose(2, 0, 1, 3, 4)
    ps = q_pos.reshape(nc, qc)

    def one(args):
        qb, ib, okb, pb = args
        pos = (ib[..., None] * SEL_BLOCK + jnp.arange(SEL_BLOCK)).reshape(B, N_KV_HEADS, qc, -1)
        mask = jnp.repeat(okb, SEL_BLOCK, axis=-1) & (pos <= pb[None, None, :, None])
        rows = row_fn(pos)
        s = jnp.einsum('bqhgd,bhqld->bhgql', qb, rows[..., 0, :]).astype(jnp.float32) * SCALE
        p = masked_softmax(s, mask[:, :, None])
        return jnp.einsum('bhgql,bhqld->bqhgd', p.astype(rows.dtype), rows[..., 1, :])

    o = lax.map(one, (qs, ids, oks, ps))
    return o.transpose(1, 0, 2, 3, 4, 5).reshape(B, Tq, N_KV_HEADS, GROUP, HEAD_DIM)


def win_prompt(q, kv_w):
    B, T = q.shape[:2]
    nb = T // Q_BLOCK
    wk = WINDOW + Q_BLOCK
    padded = jnp.concatenate([jnp.zeros((B, WINDOW) + kv_w.shape[2:], kv_w.dtype), kv_w], axis=1)
    kidx = jnp.arange(nb)[:, None] * Q_BLOCK + jnp.arange(wk)[None, :]
    kb = padded[:, kidx]
    qpos = jnp.arange(nb)[:, None] * Q_BLOCK + jnp.arange(Q_BLOCK)[None, :]
    kpos = kidx - WINDOW
    mask = window_mask(qpos[:, :, None], kpos[:, None, :])
    qb = q.reshape(B, nb, Q_BLOCK, N_KV_HEADS, GROUP, HEAD_DIM)
    o, _ = attend(qb, kb[..., 0, :], kb[..., 1, :], mask)
    return o.reshape(B, T, N_KV_HEADS, GROUP, HEAD_DIM)


def nsa_prompt(q, kv_c, kv_s, kv_w, phi_pos, phi_w1, phi_w2):
    B, T = q.shape[:2]
    win_cache = min(WINDOW, PAST_LEN)
    q_pos = jnp.arange(T)
    o_c, imp = cmp_branch(q, q_pos, kv_c, phi_pos, phi_w1, phi_w2)
    idx, ok = select_blocks(imp, q_pos, -(-T // SEL_BLOCK))
    kv_t = kv_s.transpose(0, 2, 1, 3, 4)
    o_s = slc_attend(q, q_pos, idx, ok, functools.partial(rows_from_seq, kv_t))
    o_w = win_prompt(q, kv_w)
    full = jnp.concatenate([jnp.zeros((B, win_cache) + kv_w.shape[2:], kv_w.dtype), kv_w], axis=1)
    return o_c, o_s, o_w, full[:, full.shape[1] - win_cache:]


def nsa_sample(q, kv_c, kv_s, kv_w, cache_c, cache_s, page_table, win_buf, phi_pos, phi_w1, phi_w2):
    DB, Tq = q.shape[:2]
    n_pages = page_table.shape[1]
    past_len = n_pages * PAGE_SIZE
    win_cache = win_buf.shape[1]
    q_pos = past_len + jnp.arange(Tq)
    past_c = cache_c[page_table].reshape(DB, past_len, N_KV_HEADS, 2, HEAD_DIM)
    full_c = jnp.concatenate([past_c, kv_c], axis=1)
    o_c, imp = cmp_branch(q, q_pos, full_c, phi_pos, phi_w1, phi_w2)
    idx, ok = select_blocks(imp, q_pos, -(-(past_len + Tq) // SEL_BLOCK))
    o_s = slc_attend(q, q_pos, idx, ok, functools.partial(rows_from_pages, cache_s, page_table, kv_s))
    keys = jnp.concatenate([win_buf, kv_w], axis=1)
    kpos = past_len - win_cache + jnp.arange(keys.shape[1])
    mask = window_mask(q_pos[:, None], kpos[None, :])
    o_w, _ = attend(q.reshape(DB, Tq, N_KV_HEADS, GROUP, HEAD_DIM), keys[..., 0, :], keys[..., 1, :], mask)
    return o_c, o_s, o_w, keys[:, keys.shape[1] - win_cache:]


def block(x, p_l, pos, conv_buf, ffn_buf, nsa_fn, norm_mix_g, w_in, conv_w, conv_b, conv_ln_g, conv_ln_b,
          w_out, norm_ffn_g, w_ffn_gate, w_ffn_up, ffn_conv_w, ffn_conv_b, w_ffn_down, w_ple, w_ple_gate):
    B, T, _ = x.shape
    n = rmsnorm(x, norm_mix_g)
    u, q, kv_c, kv_s, kv_w, gates = mixer_inputs(n, w_in, pos)
    cv, new_conv = causal_dwconv(u, conv_buf, conv_w, conv_b)
    cv = jax.nn.silu(layernorm(cv, conv_ln_g, conv_ln_b))
    o_c, o_s, o_w, new_win = nsa_fn(q, kv_c, kv_s, kv_w)
    g = gates[..., None]
    o = g[:, :, 0] * o_c + g[:, :, 1] * o_s + g[:, :, 2] * o_w
    mix = jnp.concatenate([cv, o.reshape(B, T, Q_COLS)], axis=-1)
    x = x + mix @ w_out
    n2 = rmsnorm(x, norm_ffn_g)
    gate, new_ffn = causal_dwconv(n2 @ w_ffn_gate, ffn_buf, ffn_conv_w, ffn_conv_b)
    x = x + (jax.nn.silu(gate) * (n2 @ w_ffn_up)) @ w_ffn_down
    x = x + jax.nn.sigmoid(x @ w_ple_gate) * (p_l @ w_ple)
    return x, (kv_c, kv_s, new_win, new_conv, new_ffn)


def setup_inputs(seed: int = 0) -> dict:
    key = jax.random.key(seed)
    ks = jax.random.split(key, 32)
    n_pages = PAST_LEN // PAGE_SIZE
    n_pool = (DEC_BATCH * n_pages * 5) // 4
    win_cache = min(WINDOW, PAST_LEN)

    def nrm(k, shape, scale):
        return scale * jax.random.normal(k, shape, jnp.float32)

    page_table = jax.random.permutation(ks[7], n_pool)[:DEC_BATCH * n_pages]
    page_table = page_table.reshape(DEC_BATCH, n_pages).astype(jnp.int32)
    return {
        'x_prompt': nrm(ks[0], (BATCH, SEQ, D_MODEL), 1.0),
        'x_sample': nrm(ks[1], (DEC_BATCH, DEC_SEQ, D_MODEL), 1.0),
        'cache_cmp_kv': nrm(ks[2], (DEPTH, n_pool, PAGE_SIZE, N_KV_HEADS, 2, HEAD_DIM), 1.0),
        'cache_slc_kv': nrm(ks[3], (DEPTH, n_pool, PAGE_SIZE, N_KV_HEADS, 2, HEAD_DIM), 1.0),
        'state_win_kv': nrm(ks[4], (DEPTH, DEC_BATCH, win_cache, N_KV_HEADS, 2, HEAD_DIM), 1.0),
        'state_conv': nrm(ks[5], (DEPTH, DEC_BATCH, CONV_WIDTH - 1, CONV_CH), 0.5),
        'state_ffn': nrm(ks[6], (DEPTH, DEC_BATCH, FFN_CONV_WIDTH - 1, FFN_DIM), 1.0),
        'page_table': page_table,
        'p_prompt': nrm(ks[8], (DEPTH, BATCH, SEQ, PLE_DIM), 1.0),
        'p_sample': nrm(ks[9], (DEPTH, DEC_BATCH, DEC_SEQ, PLE_DIM), 1.0),
        'norm_mix_g': 1.0 + nrm(ks[10], (DEPTH, D_MODEL), 0.02),
        'w_in': nrm(ks[11], (DEPTH, D_MODEL, W_IN_COLS), D_MODEL ** -0.5),
        'conv_w': nrm(ks[12], (DEPTH, CONV_WIDTH, CONV_CH), CONV_WIDTH ** -0.5),
        'conv_b': nrm(ks[13], (DEPTH, CONV_CH), 0.01),
        'conv_ln_g': 1.0 + nrm(ks[14], (DEPTH, CONV_CH), 0.02),
        'conv_ln_b': nrm(ks[15], (DEPTH, CONV_CH), 0.01),
        'phi_pos': nrm(ks[16], (DEPTH, 2, CMP_BLOCK, HEAD_DIM), 0.1),
        'phi_w1': nrm(ks[17], (DEPTH, 2, CMP_BLOCK, HEAD_DIM, CMP_HIDDEN), (CMP_BLOCK * HEAD_DIM) ** -0.5),
        'phi_w2': nrm(ks[18], (DEPTH, 2, CMP_HIDDEN, HEAD_DIM), CMP_HIDDEN ** -0.5),
        'w_out': nrm(ks[19], (DEPTH, MIX_WIDTH, D_MODEL), MIX_WIDTH ** -0.5),
        'norm_ffn_g': 1.0 + nrm(ks[20], (DEPTH, D_MODEL), 0.02),
        'w_ffn_gate': nrm(ks[21], (DEPTH, D_MODEL, FFN_DIM), D_MODEL ** -0.5),
        'w_ffn_up': nrm(ks[22], (DEPTH, D_MODEL, FFN_DIM), D_MODEL ** -0.5),
        'ffn_conv_w': nrm(ks[23], (DEPTH, FFN_CONV_WIDTH, FFN_DIM), FFN_CONV_WIDTH ** -0.5),
        'ffn_conv_b': nrm(ks[24], (DEPTH, FFN_DIM), 0.01),
        'w_ffn_down': nrm(ks[25], (DEPTH, FFN_DIM, D_MODEL), FFN_DIM ** -0.5),
        'w_ple': nrm(ks[26], (DEPTH, PLE_DIM, D_MODEL), PLE_DIM ** -0.5),
        'w_ple_gate': nrm(ks[27], (DEPTH, D_MODEL, D_MODEL), D_MODEL ** -0.5),
        'norm_final_g': 1.0 + nrm(ks[28], (D_MODEL,), 0.02),
    }


def reference(x_prompt, x_sample, cache_cmp_kv, cache_slc_kv, state_win_kv, state_conv, state_ffn,
              page_table, p_prompt, p_sample, norm_mix_g, w_in, conv_w, conv_b, conv_ln_g, conv_ln_b,
              phi_pos, phi_w1, phi_w2, w_out, norm_ffn_g, w_ffn_gate, w_ffn_up, ffn_conv_w, ffn_conv_b,
              w_ffn_down, w_ple, w_ple_gate, norm_final_g):
    B = x_prompt.shape[0]
    pos_p = jnp.arange(x_prompt.shape[1])
    pos_s = page_table.shape[1] * PAGE_SIZE + jnp.arange(x_sample.shape[1])
    hp, hs = x_prompt, x_sample
    sts_p, sts_s = [], []
    for l in range(DEPTH):
        lw = (norm_mix_g[l], w_in[l], conv_w[l], conv_b[l], conv_ln_g[l], conv_ln_b[l], w_out[l],
              norm_ffn_g[l], w_ffn_gate[l], w_ffn_up[l], ffn_conv_w[l], ffn_conv_b[l], w_ffn_down[l],
              w_ple[l], w_ple_gate[l])
        nsa_p = functools.partial(nsa_prompt, phi_pos=phi_pos[l], phi_w1=phi_w1[l], phi_w2=phi_w2[l])
        nsa_s = functools.partial(nsa_sample, cache_c=cache_cmp_kv[l], cache_s=cache_slc_kv[l],
                                  page_table=page_table, win_buf=state_win_kv[l],
                                  phi_pos=phi_pos[l], phi_w1=phi_w1[l], phi_w2=phi_w2[l])
        conv0 = jnp.zeros((B, CONV_WIDTH - 1, CONV_CH), x_prompt.dtype)
        ffn0 = jnp.zeros((B, FFN_CONV_WIDTH - 1, FFN_DIM), x_prompt.dtype)
        hp, st_p = block(hp, p_prompt[l], pos_p, conv0, ffn0, nsa_p, *lw)
        hs, st_s = block(hs, p_sample[l], pos_s, state_conv[l], state_ffn[l], nsa_s, *lw)
        sts_p.append(st_p)
        sts_s.append(st_s)
    y_prompt = rmsnorm(hp, norm_final_g)
    y_sample = rmsnorm(hs, norm_final_g)
    p_cmp, p_slc, p_win, p_conv, p_ffn = [jnp.stack([s[i] for s in sts_p]) for i in range(5)]
    s_cmp, s_slc, s_win, s_conv, s_ffn = [jnp.stack([s[i] for s in sts_s]) for i in range(5)]
    return (y_prompt, y_sample, p_cmp, p_slc, p_win, p_conv, p_ffn, s_cmp, s_slc, s_win, s_conv, s_ffn)
```

```python
import functools

import jax
import jax.numpy as jnp
from jax import lax
from jax.experimental import pallas as pl
from jax.experimental.pallas import tpu as pltpu

f32 = jnp.float32
bf16 = jnp.bfloat16

D_MODEL = 2048
HEAD_DIM = 128
N_ATTN_HEADS = 8
N_KV_HEADS = 2
GROUP = N_ATTN_HEADS // N_KV_HEADS
CONV_CH = 1024
CONV_WIDTH = 31
ROPE_DIM = HEAD_DIM // 4
ROPE_THETA = 500000.0
CMP_BLOCK = 32
CMP_STRIDE = 16
CMP_HIDDEN = 256
SEL_BLOCK = 64
SEL_SHIFT = 6
N_SEL = 16
WINDOW = 512
FFN_DIM = 5632
PLE_DIM = 256
PAGE_SIZE = 128
EPS = 1e-6
SCALE = HEAD_DIM ** -0.5
NEG = -1e30
FORCED = 1e9
Q_COLS = N_ATTN_HEADS * HEAD_DIM
KV_BRANCH_COLS = 2 * N_KV_HEADS * HEAD_DIM
LANES = 128
M_INIT = -1e37

_NT = (((1,), (1,)), ((), ()))


def _cparams(sem, vmem_mb):
    return pltpu.CompilerParams(dimension_semantics=sem, vmem_limit_bytes=vmem_mb * 1024 * 1024)


def _rmsnorm_body(x_ref, g_ref, o_ref):
    x = x_ref[...]
    ms = jnp.mean(x * x, axis=-1, keepdims=True)
    o_ref[...] = (x * lax.rsqrt(ms + EPS) * g_ref[...]).astype(o_ref.dtype)


def rmsnorm_call(x, g, out_dtype, tm):
    M, D = x.shape
    return pl.pallas_call(
        _rmsnorm_body,
        out_shape=jax.ShapeDtypeStruct((M, D), out_dtype),
        grid=(M // tm,),
        in_specs=[pl.BlockSpec((tm, D), lambda i: (i, 0)), pl.BlockSpec((1, D), lambda i: (0, 0))],
        out_specs=pl.BlockSpec((tm, D), lambda i: (i, 0)),
        compiler_params=_cparams(("arbitrary",), 32),
    )(x, g.reshape(1, D))


def _cast_rows(w_ref, wb_ref, K):
    ck = 256

    def step(i, c):
        r = pl.multiple_of(i * ck, ck)
        wb_ref[pl.ds(r, ck), :] = w_ref[pl.ds(r, ck), :].astype(bf16)
        return c

    lax.fori_loop(0, K // ck, step, 0)


def fused_mm(a_arrays, w_specs, extras, outs, epilogue, *, M, tm, tn, n_tiles, scratch=(), vmem_mb=48):
    na, nw, ne, no = len(a_arrays), len(w_specs), len(extras), len(outs)

    def body(*refs):
        a_refs = refs[:na]
        w_refs = refs[na:na + nw]
        e_refs = refs[na + nw:na + nw + ne]
        o_refs = refs[na + nw + ne:na + nw + ne + no]
        wb_refs = refs[na + nw + ne + no:na + nw + ne + no + nw]
        s_refs = refs[na + nw + ne + no + nw:]

        @pl.when(pl.program_id(1) == 0)
        def _():
            for (_, _, K, _, _), w, wb in zip(w_specs, w_refs, wb_refs):
                _cast_rows(w, wb, K)

        a_vals = [a[...].astype(bf16) for a in a_refs]
        accs = [jnp.dot(a_vals[ai], wb[...], preferred_element_type=f32)
                for (_, ai, _, _, _), wb in zip(w_specs, wb_refs)]
        epilogue(accs, e_refs, o_refs, s_refs)

    in_specs = [pl.BlockSpec((tm, a.shape[1]), lambda n, m: (m, 0)) for a in a_arrays]
    for (_, _, K, rb, cb) in w_specs:
        in_specs.append(pl.BlockSpec((K, tn), functools.partial(lambda n, m, rb, cb: (rb, n + cb), rb=rb, cb=cb)))
    in_specs += [pl.BlockSpec(bs, im) for (_, bs, im) in extras]
    out_specs = [pl.BlockSpec(bs, im) for (_, bs, im) in outs]
    scratch_shapes = [pltpu.VMEM((K, tn), bf16) for (_, _, K, _, _) in w_specs] + list(scratch)
    res = pl.pallas_call(
        body,
        out_shape=[o for (o, _, _) in outs],
        grid=(n_tiles, M // tm),
        in_specs=in_specs,
        out_specs=out_specs,
        scratch_shapes=scratch_shapes,
        compiler_params=_cparams(("arbitrary", "arbitrary"), vmem_mb),
    )(*a_arrays, *[w for (w, _, _, _, _) in w_specs], *[e for (e, _, _) in extras])
    return res


def _rope(x, c, s):
    half = ROPE_DIM // 2
    lane = lax.broadcasted_iota(jnp.int32, x.shape, 1)
    partner = jnp.where(lane < half, pltpu.roll(x, LANES - half, 1), pltpu.roll(x, half, 1))
    return x * c + partner * s


def rope_tables(pos):
    half = ROPE_DIM // 2
    inv = ROPE_THETA ** (-jnp.arange(half, dtype=f32) / half)
    ang = pos.astype(f32)[:, None] * inv[None, :]
    cos, sin = jnp.cos(ang), jnp.sin(ang)
    T = pos.shape[0]
    c = jnp.concatenate([cos, cos, jnp.ones((T, HEAD_DIM - ROPE_DIM), f32)], axis=1)
    s = jnp.concatenate([-sin, sin, jnp.zeros((T, HEAD_DIM - ROPE_DIM), f32)], axis=1)
    return c, s


def mixer_proj(n, w_in, w_gates, cos_t, sin_t, tm):
    M = n.shape[0]
    tn = 512

    def glu_epi(accs, e, o, s):
        o[0][...] = accs[0] * jax.nn.sigmoid(accs[1])

    (u,) = fused_mm([n], [(w_in, 0, D_MODEL, 0, 0), (w_in, 0, D_MODEL, 0, CONV_CH // tn)], [],
                    [(jax.ShapeDtypeStruct((M, CONV_CH), f32), (tm, tn), lambda n_, m: (m, n_))],
                    glu_epi, M=M, tm=tm, tn=tn, n_tiles=CONV_CH // tn)

    rope_extras = [(cos_t, (tm, LANES), lambda n_, m: (m, 0)), (sin_t, (tm, LANES), lambda n_, m: (m, 0))]

    def q_epi(accs, e, o, s):
        c, sn = e[0][...], e[1][...]
        for hb in range(tn // LANES):
            o[0][:, hb * LANES:(hb + 1) * LANES] = _rope(accs[0][:, hb * LANES:(hb + 1) * LANES], c, sn)

    (q,) = fused_mm([n], [(w_in, 0, D_MODEL, 0, 2 * CONV_CH // tn)], rope_extras,
                    [(jax.ShapeDtypeStruct((M, Q_COLS), f32), (tm, tn), lambda n_, m: (m, n_))],
                    q_epi, M=M, tm=tm, tn=tn, n_tiles=Q_COLS // tn)

    def kv_epi(accs, e, o, s):
        c, sn = e[0][...], e[1][...]
        a = accs[0]
        for h in range(N_KV_HEADS):
            k = _rope(a[:, h * LANES:(h + 1) * LANES], c, sn)
            v = a[:, (N_KV_HEADS + h) * LANES:(N_KV_HEADS + h + 1) * LANES]
            for dst in o:
                dst[:, (2 * h) * LANES:(2 * h + 1) * LANES] = k.astype(dst.dtype)
                dst[:, (2 * h + 1) * LANES:(2 * h + 2) * LANES] = v.astype(dst.dtype)

    kv, kvb = fused_mm([n], [(w_in, 0, D_MODEL, 0, (2 * CONV_CH + Q_COLS) // tn)], rope_extras,
                       [(jax.ShapeDtypeStruct((3, M, KV_BRANCH_COLS), dt), (None, tm, tn), lambda n_, m: (n_, m, 0))
                        for dt in (f32, bf16)],
                       kv_epi, M=M, tm=tm, tn=tn, n_tiles=3)

    def gate_epi(accs, e, o, s):
        o[0][...] = jax.nn.sigmoid(accs[0])

    (gates,) = fused_mm([n], [(w_gates, 0, D_MODEL, 0, 0)], [],
                        [(jax.ShapeDtypeStruct((M, LANES), f32), (tm, LANES), lambda n_, m: (m, 0))],
                        gate_epi, M=M, tm=tm, tn=LANES, n_tiles=1)
    return u, q, kv, kvb, gates


def _ln_silu(y, g, b):
    mu = jnp.mean(y, axis=-1, keepdims=True)
    yc = y - mu
    var = jnp.mean(yc * yc, axis=-1, keepdims=True)
    return jax.nn.silu(yc * lax.rsqrt(var + EPS) * g + b)


def _conv_prompt_body(u_ref, halo_ref, buf_ref, w_ref, b_ref, lg_ref, lb_ref, o_ref, f_sc, y_sc, *, tm):
    i = pl.program_id(0)
    HALO = 32
    off = HALO - (CONV_WIDTH - 1)

    @pl.when(i == 0)
    def _():
        f_sc[0:HALO, :] = buf_ref[...]

    @pl.when(i > 0)
    def _():
        f_sc[0:HALO, :] = halo_ref[...]

    f_sc[HALO:HALO + tm, :] = u_ref[...]
    cw, rw = 256, 64
    for c0 in range(0, CONV_CH, cw):
        for r0 in range(0, tm, rw):
            acc = jnp.broadcast_to(b_ref[:, c0:c0 + cw], (rw, cw))
            for k in range(CONV_WIDTH):
                acc = acc + w_ref[k:k + 1, c0:c0 + cw] * f_sc[pl.ds(r0 + k + off, rw), c0:c0 + cw]
            y_sc[r0:r0 + rw, c0:c0 + cw] = acc
    o_ref[...] = _ln_silu(y_sc[...], lg_ref[...], lb_ref[...]).astype(o_ref.dtype)


def conv_prompt(u, buf, conv_w, conv_b, ln_g, ln_b, tm=128):
    T, C = u.shape
    HALO = 32
    bufp = jnp.concatenate([jnp.zeros((HALO - (CONV_WIDTH - 1), C), f32), buf], axis=0)
    wp = jnp.concatenate([conv_w, jnp.zeros((HALO - CONV_WIDTH, C), f32)], axis=0)
    hb = tm // HALO
    return pl.pallas_call(
        functools.partial(_conv_prompt_body, tm=tm),
        out_shape=jax.ShapeDtypeStruct((T, C), bf16),
        grid=(T // tm,),
        in_specs=[pl.BlockSpec((tm, C), lambda i: (i, 0)),
                  pl.BlockSpec((HALO, C), lambda i: (jnp.maximum(i * hb - 1, 0), 0)),
                  pl.BlockSpec((HALO, C), lambda i: (0, 0)),
                  pl.BlockSpec((HALO, C), lambda i: (0, 0)),
                  pl.BlockSpec((1, C), lambda i: (0, 0)),
                  pl.BlockSpec((1, C), lambda i: (0, 0)),
                  pl.BlockSpec((1, C), lambda i: (0, 0))],
        out_specs=pl.BlockSpec((tm, C), lambda i: (i, 0)),
        scratch_shapes=[pltpu.VMEM((HALO + tm, C), f32), pltpu.VMEM((tm, C), f32)],
        compiler_params=_cparams(("arbitrary",), 32),
    )(u, u, bufp, wp, conv_b.reshape(1, C), ln_g.reshape(1, C), ln_b.reshape(1, C))


def _conv_sample_body(u_ref, st_ref, w_ref, b_ref, lg_ref, lb_ref, o_ref, y_sc, *, tb):
    nb = CONV_WIDTH - 1
    w_past = w_ref[0:nb, :]
    for b in range(tb):
        y_sc[b:b + 1, :] = jnp.sum(st_ref[b] * w_past, axis=0, keepdims=True)
    y = y_sc[...] + w_ref[nb:nb + 1, :] * u_ref[...] + b_ref[...]
    o_ref[...] = _ln_silu(y, lg_ref[...], lb_ref[...]).astype(o_ref.dtype)


def conv_sample(u, state, conv_w, conv_b, ln_g, ln_b, tb=16):
    B, C = u.shape
    nb = CONV_WIDTH - 1
    return pl.pallas_call(
        functools.partial(_conv_sample_body, tb=tb),
        out_shape=jax.ShapeDtypeStruct((B, C), bf16),
        grid=(B // tb,),
        in_specs=[pl.BlockSpec((tb, C), lambda i: (i, 0)),
                  pl.BlockSpec((tb, nb, C), lambda i: (i, 0, 0)),
                  pl.BlockSpec((CONV_WIDTH, C), lambda i: (0, 0)),
                  pl.BlockSpec((1, C), lambda i: (0, 0)),
                  pl.BlockSpec((1, C), lambda i: (0, 0)),
                  pl.BlockSpec((1, C), lambda i: (0, 0))],
        out_specs=pl.BlockSpec((tb, C), lambda i: (i, 0)),
        scratch_shapes=[pltpu.VMEM((tb, C), f32)],
        compiler_params=_cparams(("arbitrary",), 32),
    )(u, state, conv_w, conv_b.reshape(1, C), ln_g.reshape(1, C), ln_b.reshape(1, C))


def _phi_tail(pq, posb, w2):
    n = pq.shape[0]
    h1 = pq[:, :CMP_HIDDEN] + pltpu.roll(pq[:, CMP_HIDDEN:], n - 1, 0) + posb
    return jnp.dot(jax.nn.gelu(h1).astype(bf16), w2, preferred_element_type=f32)


def _pos_bias(pos_ref, w1_ref):
    half = CMP_STRIDE * HEAD_DIM
    pb = pos_ref[...].astype(bf16)
    r = jnp.dot(pb[:, :half], w1_ref[:, :CMP_HIDDEN], preferred_element_type=f32)
    r = r + jnp.dot(pb[:, half:], w1_ref[:, CMP_HIDDEN:], preferred_element_type=f32)
    return r[0:1, :]


def _compress_prompt_body(x_ref, w1_ref, pos_ref, w2_ref, o_ref, *, nchunk):
    pieces = [x_ref[pl.ds(r, nchunk, stride=CMP_STRIDE), :].astype(bf16) for r in range(CMP_STRIDE)]
    scat = jnp.concatenate(pieces, axis=1)
    pq = jnp.dot(scat, w1_ref[...], preferred_element_type=f32)
    o_ref[...] = _phi_tail(pq, _pos_bias(pos_ref, w1_ref), w2_ref[...])


def compress_prompt(kv_c, w1cat, pos8, w2b):
    T = kv_c.shape[0]
    nchunk = T // CMP_STRIDE
    return pl.pallas_call(
        functools.partial(_compress_prompt_body, nchunk=nchunk),
        out_shape=jax.ShapeDtypeStruct((2 * N_KV_HEADS, nchunk, HEAD_DIM), f32),
        grid=(2 * N_KV_HEADS,),
        in_specs=[pl.BlockSpec((T, LANES), lambda c: (0, c)),
                  pl.BlockSpec((None, CMP_STRIDE * HEAD_DIM, 2 * CMP_HIDDEN), lambda c: (c % 2, 0, 0)),
                  pl.BlockSpec((None, 8, CMP_BLOCK * HEAD_DIM), lambda c: (c % 2, 0, 0)),
                  pl.BlockSpec((None, CMP_HIDDEN, HEAD_DIM), lambda c: (c % 2, 0, 0))],
        out_specs=pl.BlockSpec((None, nchunk, HEAD_DIM), lambda c: (c, 0, 0)),
        compiler_params=_cparams(("arbitrary",), 48),
    )(kv_c, w1cat, pos8, w2b)


def _compress_sample_body(pt_ref, *refs, n_pages):
    page_refs = refs[:n_pages]
    w1_ref, pos_ref, w2_ref, o_ref, scat_sc = refs[n_pages:]
    cpp = PAGE_SIZE // CMP_STRIDE
    nchunk = n_pages * cpp
    for j in range(2):
        for h in range(N_KV_HEADS):
            nblk = KV_BRANCH_COLS // LANES
            for p in range(n_pages):
                for r in range(CMP_STRIDE):
                    col = (r * nblk + 2 * h + j) * LANES
                    scat_sc[h * nchunk + p * cpp:h * nchunk + (p + 1) * cpp, r * LANES:(r + 1) * LANES] = (
                        page_refs[p][:, col:col + LANES])
        pq = jnp.dot(scat_sc[...].astype(bf16), w1_ref[j], preferred_element_type=f32)
        res = _phi_tail(pq, _pos_bias(pos_ref.at[j], w1_ref.at[j]), w2_ref[j])
        for h in range(N_KV_HEADS):
            o_ref[2 * h + j] = res[h * nchunk:(h + 1) * nchunk, :]


def compress_sample(cache2d, page_table, w1cat, pos8, w2b):
    B, n_pages = page_table.shape
    cpp = PAGE_SIZE // CMP_STRIDE
    nchunk = n_pages * cpp
    cache2d = cache2d.reshape(cache2d.shape[0], cpp, CMP_STRIDE * KV_BRANCH_COLS)
    page_specs = [pl.BlockSpec((None, cpp, CMP_STRIDE * KV_BRANCH_COLS),
                               functools.partial(lambda b, pt, p: (pt[b, p], 0, 0), p=p)) for p in range(n_pages)]
    grid_spec = pltpu.PrefetchScalarGridSpec(
        num_scalar_prefetch=1,
        grid=(B,),
        in_specs=page_specs + [
            pl.BlockSpec((2, CMP_STRIDE * HEAD_DIM, 2 * CMP_HIDDEN), lambda b, pt: (0, 0, 0)),
            pl.BlockSpec((2, 8, CMP_BLOCK * HEAD_DIM), lambda b, pt: (0, 0, 0)),
            pl.BlockSpec((2, CMP_HIDDEN, HEAD_DIM), lambda b, pt: (0, 0, 0))],
        out_specs=pl.BlockSpec((None, 2 * N_KV_HEADS, nchunk, HEAD_DIM), lambda b, pt: (b, 0, 0, 0)),
        scratch_shapes=[pltpu.VMEM((N_KV_HEADS * nchunk, CMP_STRIDE * HEAD_DIM), f32)],
    )
    return pl.pallas_call(
        functools.partial(_compress_sample_body, n_pages=n_pages),
        out_shape=jax.ShapeDtypeStruct((B, 2 * N_KV_HEADS, nchunk, HEAD_DIM), f32),
        grid_spec=grid_spec,
        compiler_params=_cparams(("arbitrary",), 48),
    )(page_table, *([cache2d] * n_pages), w1cat, pos8, w2b)


def _dot3(x, ov):
    hi = x.astype(bf16)
    r1 = x - hi.astype(f32)
    mid = r1.astype(bf16)
    lo = (r1 - mid.astype(f32)).astype(bf16)
    return (jnp.dot(hi, ov, preferred_element_type=f32) + jnp.dot(mid, ov, preferred_element_type=f32)
            + jnp.dot(lo, ov, preferred_element_type=f32))


def _select_bias(score, cur, n_sel):
    jj = lax.broadcasted_iota(jnp.int32, score.shape, 1)
    valid = (jj <= cur) & (jj < n_sel)
    forced = (jj == 0) | (jj == cur) | (jj == cur - 1)
    score = jnp.where(valid, jnp.where(forced, FORCED, score), NEG)
    score = jnp.where(jj < n_sel, score, -jnp.inf)

    def pick_one(_, carry):
        sc, sel = carry
        mx = jnp.max(sc, axis=-1, keepdims=True)
        idx = jnp.min(jnp.where(sc == mx, jj, LANES), axis=-1, keepdims=True)
        pick = jj == idx
        sel = jnp.where(pick & (mx > NEG / 2), 1.0, sel)
        sc = jnp.where(pick, -jnp.inf, sc)
        return sc, sel

    _, sel = lax.fori_loop(0, min(N_SEL, n_sel), pick_one, (score, jnp.zeros(score.shape, f32)))
    return jnp.where(sel > 0, 0.0, NEG)


def overlap_matrix(n_rows, n_cmp):
    i = jnp.arange(n_rows)[:, None] * CMP_STRIDE
    j = jnp.arange(LANES)[None, :] * SEL_BLOCK
    ov = (i < j + SEL_BLOCK) & (i + CMP_BLOCK > j) & (jnp.arange(n_rows)[:, None] < n_cmp)
    return ov.astype(bf16)


def _cmp_prompt_body(q_ref, kvc_ref, ov_ref, oc_ref, sel_ref, *, tq, n_cmp, n_sel):
    i = pl.program_id(0)
    ncp = kvc_ref.shape[1]
    qpos = i * tq + lax.broadcasted_iota(jnp.int32, (tq, 1), 0)
    c = lax.broadcasted_iota(jnp.int32, (1, ncp), 1)
    mask = (c < n_cmp) & (c * CMP_STRIDE + (CMP_BLOCK - 1) <= qpos)
    for h in range(N_KV_HEADS):
        kc = kvc_ref[2 * h].astype(bf16)
        vc = kvc_ref[2 * h + 1].astype(bf16)
        imp = jnp.zeros((tq, ncp), f32)
        for g in range(GROUP):
            hd = GROUP * h + g
            qg = q_ref[:, hd * LANES:(hd + 1) * LANES].astype(bf16)
            s = lax.dot_general(qg, kc, _NT, preferred_element_type=f32) * SCALE
            s = jnp.where(mask, s, NEG)
            m = jnp.max(s, axis=-1, keepdims=True)
            e = jnp.where(mask, jnp.exp(s - m), 0.0)
            p = e / jnp.maximum(jnp.sum(e, axis=-1, keepdims=True), 1e-30)
            oc_ref[:, hd * LANES:(hd + 1) * LANES] = jnp.dot(p.astype(bf16), vc, preferred_element_type=f32)
            imp = imp + p
        score = _dot3(imp, ov_ref[...])
        sel_ref[h] = _select_bias(score, qpos >> SEL_SHIFT, n_sel)


def cmp_prompt(q, kvc, tq=256):
    T = q.shape[0]
    ncp = kvc.shape[1]
    n_cmp = (T - CMP_BLOCK) // CMP_STRIDE + 1
    n_sel = -(-T // SEL_BLOCK)
    assert n_sel <= LANES
    ov = overlap_matrix(ncp, n_cmp)
    return pl.pallas_call(
        functools.partial(_cmp_prompt_body, tq=tq, n_cmp=n_cmp, n_sel=n_sel),
        out_shape=[jax.ShapeDtypeStruct((T, Q_COLS), f32), jax.ShapeDtypeStruct((N_KV_HEADS, T, LANES), f32)],
        grid=(T // tq,),
        in_specs=[pl.BlockSpec((tq, Q_COLS), lambda i: (i, 0)),
                  pl.BlockSpec(kvc.shape, lambda i: (0, 0, 0)),
                  pl.BlockSpec(ov.shape, lambda i: (0, 0))],
        out_specs=[pl.BlockSpec((tq, Q_COLS), lambda i: (i, 0)),
                   pl.BlockSpec((N_KV_HEADS, tq, LANES), lambda i: (0, i, 0))],
        compiler_params=_cparams(("arbitrary",), 48),
    )(q, kvc, ov)


def _flash_update(s, v, m_sc, l_sc, acc_sc):
    m_prev = m_sc[...]
    m_new = jnp.maximum(m_prev, jnp.max(s, axis=-1, keepdims=True))
    alpha = jnp.exp(m_prev - m_new)
    p = jnp.exp(s - m_new)
    l_sc[...] = alpha * l_sc[...] + jnp.sum(p, axis=-1, keepdims=True)
    acc_sc[...] = alpha * acc_sc[...] + jnp.dot(p.astype(bf16), v, preferred_element_type=f32)
    m_sc[...] = m_new


def _flash_reset(m_sc, l_sc, acc_sc):
    m_sc[...] = jnp.full(m_sc.shape, M_INIT, f32)
    l_sc[...] = jnp.zeros(l_sc.shape, f32)
    acc_sc[...] = jnp.zeros(acc_sc.shape, f32)


def _attn_prompt_body(q_ref, sel_ref, ks_ref, vs_ref, kw_ref, vw_ref, oc_ref, g_ref, o_ref,
                      m_sc, l_sc, acc_sc, *, tq, tk, tkw):
    h = pl.program_id(0)
    i = pl.program_id(1)
    q0 = i * tq
    R = GROUP * tq
    qs = jnp.concatenate([q_ref[:, g * LANES:(g + 1) * LANES] for g in range(GROUP)], axis=0)
    qs = (qs * SCALE).astype(bf16)
    selb = sel_ref[...].astype(bf16)
    qaug = jnp.concatenate([qs, jnp.concatenate([selb] * GROUP, axis=0)], axis=1)
    qpos = q0 + (lax.broadcasted_iota(jnp.int32, (R, 1), 0) & (tq - 1))

    _flash_reset(m_sc, l_sc, acc_sc)

    def slc_step(j, carry):
        k0 = pl.multiple_of(j * tk, tk)
        k = ks_ref[pl.ds(k0, tk), :].astype(bf16)
        kblk = (k0 + lax.broadcasted_iota(jnp.int32, (tk, LANES), 0)) >> SEL_SHIFT
        onehot = jnp.where(kblk == lax.broadcasted_iota(jnp.int32, (tk, LANES), 1), 1.0, 0.0).astype(bf16)
        s = lax.dot_general(qaug, jnp.concatenate([k, onehot], axis=1), _NT, preferred_element_type=f32)
        kpos = k0 + lax.broadcasted_iota(jnp.int32, (1, tk), 1)
        s = jnp.where(kpos <= qpos, s, NEG)
        _flash_update(s, vs_ref[pl.ds(k0, tk), :].astype(bf16), m_sc, l_sc, acc_sc)
        return carry

    lax.fori_loop(0, (q0 + tq - 1) // tk + 1, slc_step, 0)
    o_s = acc_sc[...] / l_sc[...]

    _flash_reset(m_sc, l_sc, acc_sc)

    def win_step(j, carry):
        k0 = pl.multiple_of(j * tkw, tkw)
        k = kw_ref[pl.ds(k0, tkw), :].astype(bf16)
        s = lax.dot_general(qs, k, _NT, preferred_element_type=f32)
        kpos = k0 + lax.broadcasted_iota(jnp.int32, (1, tkw), 1)
        s = jnp.where((kpos <= qpos) & (qpos - kpos < WINDOW), s, NEG)
        _flash_update(s, vw_ref[pl.ds(k0, tkw), :].astype(bf16), m_sc, l_sc, acc_sc)
        return carry

    lax.fori_loop(jnp.maximum((q0 - WINDOW) // tkw, 0), (q0 + tq - 1) // tkw + 1, win_step, 0)
    o_w = acc_sc[...] / l_sc[...]

    gts = g_ref[...]
    lane = lax.broadcasted_iota(jnp.int32, gts.shape, 1)

    def gate(branch, g):
        col = branch * N_ATTN_HEADS + h * GROUP + g
        return jnp.sum(jnp.where(lane == col, gts, 0.0), axis=-1, keepdims=True)

    for g in range(GROUP):
        rows = slice(g * tq, (g + 1) * tq)
        o = (gate(0, g) * oc_ref[:, g * LANES:(g + 1) * LANES] + gate(1, g) * o_s[rows] + gate(2, g) * o_w[rows])
        o_ref[:, g * LANES:(g + 1) * LANES] = o.astype(o_ref.dtype)


def attn_prompt(q, selb, kv, o_c, gates, tq=128, tk=512, tkw=128):
    T = q.shape[0]
    gw = GROUP * LANES
    R = GROUP * tq

    def kv_spec(branch, j):
        return pl.BlockSpec((None, T, LANES), lambda h, i: (branch, 0, 2 * h + j))

    return pl.pallas_call(
        functools.partial(_attn_prompt_body, tq=tq, tk=tk, tkw=tkw),
        out_shape=jax.ShapeDtypeStruct((T, Q_COLS), bf16),
        grid=(N_KV_HEADS, T // tq),
        in_specs=[pl.BlockSpec((tq, gw), lambda h, i: (i, h)),
                  pl.BlockSpec((None, tq, LANES), lambda h, i: (h, i, 0)),
                  kv_spec(1, 0), kv_spec(1, 1), kv_spec(2, 0), kv_spec(2, 1),
                  pl.BlockSpec((tq, gw), lambda h, i: (i, h)),
                  pl.BlockSpec((tq, LANES), lambda h, i: (i, 0))],
        out_specs=pl.BlockSpec((tq, gw), lambda h, i: (i, h)),
        scratch_shapes=[pltpu.VMEM((R, 1), f32), pltpu.VMEM((R, 1), f32), pltpu.VMEM((R, LANES), f32)],
        compiler_params=_cparams(("arbitrary", "arbitrary"), 56),
    )(q, selb, kv, kv, kv, kv, o_c, gates)


QROWS = 16


def _cmp_sample_body(q_ref, kvc_ref, ov_ref, oc_ref, sel_ref, imp_sc, *, tb, n_cmp, n_sel, q_pos):
    ncp = kvc_ref.shape[2]
    c = lax.broadcasted_iota(jnp.int32, (1, ncp), 1)
    mask = (c < n_cmp) & (c * CMP_STRIDE + (CMP_BLOCK - 1) <= q_pos)
    for b in range(tb):
        for h in range(N_KV_HEADS):
            qb = q_ref[b, h].astype(bf16)
            kc = kvc_ref[b, 2 * h].astype(bf16)
            vc = kvc_ref[b, 2 * h + 1].astype(bf16)
            s = lax.dot_general(qb, kc, _NT, preferred_element_type=f32) * SCALE
            s = jnp.where(mask, s, NEG)
            m = jnp.max(s, axis=-1, keepdims=True)
            e = jnp.where(mask, jnp.exp(s - m), 0.0)
            p = e / jnp.maximum(jnp.sum(e, axis=-1, keepdims=True), 1e-30)
            oc_ref[b, h] = jnp.dot(p.astype(bf16), vc, preferred_element_type=f32)
            r = N_KV_HEADS * b + h
            imp_sc[r:r + 1, :] = jnp.sum(p[0:GROUP, :], axis=0, keepdims=True)
    score = _dot3(imp_sc[...], ov_ref[...])
    cur = jnp.full((N_KV_HEADS * tb, 1), q_pos // SEL_BLOCK, jnp.int32)
    sel_ref[...] = _select_bias(score, cur, n_sel)


def cmp_sample(q16, kvc, n_cmp, n_sel, q_pos, tb=8):
    B = q16.shape[0]
    ncp = kvc.shape[2]
    ov = overlap_matrix(ncp, n_cmp)
    R = N_KV_HEADS * tb
    return pl.pallas_call(
        functools.partial(_cmp_sample_body, tb=tb, n_cmp=n_cmp, n_sel=n_sel, q_pos=q_pos),
        out_shape=[jax.ShapeDtypeStruct(q16.shape, f32), jax.ShapeDtypeStruct((B * N_KV_HEADS, LANES), f32)],
        grid=(B // tb,),
        in_specs=[pl.BlockSpec((tb,) + q16.shape[1:], lambda i: (i, 0, 0, 0)),
                  pl.BlockSpec((tb,) + kvc.shape[1:], lambda i: (i, 0, 0, 0)),
                  pl.BlockSpec(ov.shape, lambda i: (0, 0))],
        out_specs=[pl.BlockSpec((tb,) + q16.shape[1:], lambda i: (i, 0, 0, 0)),
                   pl.BlockSpec((R, LANES), lambda i: (i, 0))],
        scratch_shapes=[pltpu.VMEM((R, LANES), f32)],
        compiler_params=_cparams(("arbitrary",), 32),
    )(q16, kvc, ov)


def _attn_sample_body(pt_ref, q_ref, sel_ref, kvn_ref, oc_ref, g_ref, win_ref, *refs, n_pages, past_len):
    page_refs = refs[:n_pages]
    o_ref, wout_ref = refs[n_pages:]
    q_pos = past_len
    wc = win_ref.shape[0]
    nk = n_pages * PAGE_SIZE
    gts = g_ref[...]
    lane = lax.broadcasted_iota(jnp.int32, (QROWS, LANES), 1)
    row = lax.broadcasted_iota(jnp.int32, (QROWS, LANES), 0)

    def bf_round(x):
        return x.astype(bf16).astype(f32)

    for h in range(N_KV_HEADS):
        kcol, vcol = (2 * h) * LANES, (2 * h + 1) * LANES
        qb = (q_ref[h] * SCALE).astype(bf16)
        qf = qb.astype(f32)

        selrow = sel_ref[h:h + 1, :]
        k = jnp.concatenate([pr[:, kcol:kcol + LANES] for pr in page_refs], axis=0).astype(bf16)
        v = jnp.concatenate([pr[:, vcol:vcol + LANES] for pr in page_refs], axis=0).astype(bf16)
        kblk = lax.broadcasted_iota(jnp.int32, (nk, LANES), 0) >> SEL_SHIFT
        onehot = jnp.where(kblk == lax.broadcasted_iota(jnp.int32, (nk, LANES), 1), 1.0, 0.0).astype(bf16)
        qaug = jnp.concatenate([qb, jnp.broadcast_to(selrow, (QROWS, LANES)).astype(bf16)], axis=1)
        s = lax.dot_general(qaug, jnp.concatenate([k, onehot], axis=1), _NT, preferred_element_type=f32)
        kpos = lax.broadcasted_iota(jnp.int32, (1, nk), 1)
        s = jnp.where(kpos <= q_pos, s, NEG)
        k_new = bf_round(kvn_ref[1:2, kcol:kcol + LANES])
        v_new = bf_round(kvn_ref[1:2, vcol:vcol + LANES])
        bias_new = jnp.sum(jnp.where(lane[0:1] == past_len // SEL_BLOCK, selrow, 0.0), axis=-1, keepdims=True)
        s_new = jnp.sum(qf * k_new, axis=-1, keepdims=True) + bias_new
        m = jnp.maximum(jnp.max(s, axis=-1, keepdims=True), s_new)
        p = jnp.exp(s - m)
        p_new = jnp.exp(s_new - m)
        l = jnp.sum(p, axis=-1, keepdims=True) + p_new
        o_s = (jnp.dot(p.astype(bf16), v, preferred_element_type=f32) + bf_round(p_new) * v_new) / l

        kw = win_ref[:, kcol:kcol + LANES].astype(bf16)
        vw = win_ref[:, vcol:vcol + LANES].astype(bf16)
        s = lax.dot_general(qb, kw, _NT, preferred_element_type=f32)
        kpos = (past_len - wc) + lax.broadcasted_iota(jnp.int32, (1, wc), 1)
        s = jnp.where((kpos <= q_pos) & (q_pos - kpos < WINDOW) & (kpos >= 0), s, NEG)
        kw_new = bf_round(kvn_ref[2:3, kcol:kcol + LANES])
        vw_new = bf_round(kvn_ref[2:3, vcol:vcol + LANES])
        s_new = jnp.sum(qf * kw_new, axis=-1, keepdims=True)
        m = jnp.maximum(jnp.max(s, axis=-1, keepdims=True), s_new)
        p = jnp.exp(s - m)
        p_new = jnp.exp(s_new - m)
        l = jnp.sum(p, axis=-1, keepdims=True) + p_new
        o_w = (jnp.dot(p.astype(bf16), vw, preferred_element_type=f32) + bf_round(p_new) * vw_new) / l

        def gate(branch):
            col = branch * N_ATTN_HEADS + h * GROUP + row
            return jnp.sum(jnp.where(lane == col, gts, 0.0), axis=-1, keepdims=True)

        o_ref[h] = gate(0) * oc_ref[h] + gate(1) * o_s + gate(2) * o_w

    wrow = lax.broadcasted_iota(jnp.int32, win_ref.shape, 0)
    wout_ref[...] = jnp.where(wrow == wc - 1, kvn_ref[2:3, :], pltpu.roll(win_ref[...], wc - 1, 0))


def attn_sample(q16, selb, kv_new, o_c, gates, win2d, cache2d, page_table):
    B, n_pages = page_table.shape
    wc = win2d.shape[1]
    qblk = (None,) + q16.shape[1:]
    page_specs = [pl.BlockSpec((None, PAGE_SIZE, KV_BRANCH_COLS),
                               functools.partial(lambda b, pt, p: (pt[b, p], 0, 0), p=p)) for p in range(n_pages)]
    grid_spec = pltpu.PrefetchScalarGridSpec(
        num_scalar_prefetch=1,
        grid=(B,),
        in_specs=[pl.BlockSpec(qblk, lambda b, pt: (b, 0, 0, 0)),
                  pl.BlockSpec((None, N_KV_HEADS, LANES), lambda b, pt: (b, 0, 0)),
                  pl.BlockSpec((None, 3, KV_BRANCH_COLS), lambda b, pt: (b, 0, 0)),
                  pl.BlockSpec(qblk, lambda b, pt: (b, 0, 0, 0)),
                  pl.BlockSpec((None, 1, LANES), lambda b, pt: (b, 0, 0)),
                  pl.BlockSpec((None, wc, KV_BRANCH_COLS), lambda b, pt: (b, 0, 0))] + page_specs,
        out_specs=[pl.BlockSpec(qblk, lambda b, pt: (b, 0, 0, 0)),
                   pl.BlockSpec((None, wc, KV_BRANCH_COLS), lambda b, pt: (b, 0, 0))],
    )
    return pl.pallas_call(
        functools.partial(_attn_sample_body, n_pages=n_pages, past_len=n_pages * PAGE_SIZE),
        out_shape=[jax.ShapeDtypeStruct(q16.shape, f32), jax.ShapeDtypeStruct(win2d.shape, f32)],
        grid_spec=grid_spec,
        compiler_params=_cparams(("arbitrary",), 48),
    )(page_table, q16, selb, kv_new, o_c, gates, win2d, *([cache2d] * n_pages))


def out_proj(x, cv, o, w_out, tm):
    M = x.shape[0]
    tn = 512
    half = w_out.shape[0] // 2

    def epi(accs, e, o_, s):
        o_[0][...] = e[0][...] + accs[0] + accs[1]

    (x1,) = fused_mm([cv, o], [(w_out, 0, half, 0, 0), (w_out, 1, half, 1, 0)],
                     [(x, (tm, tn), lambda n, m: (m, n))],
                     [(jax.ShapeDtypeStruct((M, D_MODEL), f32), (tm, tn), lambda n, m: (m, n))],
                     epi, M=M, tm=tm, tn=tn, n_tiles=D_MODEL // tn)
    return x1


def ffn_hidden_prompt(n2, buf, w_gate, w_up, conv_w, conv_b, tm):
    M = n2.shape[0]
    tn = 512
    F = w_gate.shape[1]
    bufp = jnp.concatenate([jnp.zeros((6, F), f32), buf], axis=0)
    cwp = jnp.concatenate([conv_w, jnp.zeros((5, F), f32)], axis=0)

    def epi(accs, e, o, s):
        g, u = accs
        gs = s[0]

        @pl.when(pl.program_id(1) == 0)
        def _():
            gs[0:8, :] = e[0][...]

        gs[8:8 + tm, :] = g
        cw = e[1][...]
        gate = cw[0:1] * gs[pl.ds(6, tm), :] + cw[1:2] * gs[pl.ds(7, tm), :] + cw[2:3] * g + e[2][...]
        o[0][...] = (jax.nn.silu(gate) * u).astype(bf16)
        tail = g[tm - 8:tm, :]
        o[1][...] = tail
        gs[0:8, :] = tail

    h, tail = fused_mm([n2], [(w_gate, 0, D_MODEL, 0, 0), (w_up, 0, D_MODEL, 0, 0)],
                       [(bufp, (8, tn), lambda n, m: (0, n)), (cwp, (8, tn), lambda n, m: (0, n)),
                        (conv_b.reshape(1, F), (1, tn), lambda n, m: (0, n))],
                       [(jax.ShapeDtypeStruct((M, F), bf16), (tm, tn), lambda n, m: (m, n)),
                        (jax.ShapeDtypeStruct((8, F), f32), (8, tn), lambda n, m: (0, n))],
                       epi, M=M, tm=tm, tn=tn, n_tiles=F // tn,
                       scratch=[pltpu.VMEM((tm + 8, tn), f32)], vmem_mb=56)
    return h, tail


def ffn_hidden_sample(n2, state2d, w_gate, w_up, conv_w, conv_b):
    M = n2.shape[0]
    tn = 512
    F = w_gate.shape[1]
    cwp = jnp.concatenate([conv_w, jnp.zeros((5, F), f32)], axis=0)

    def epi(accs, e, o, s):
        g, u = accs
        cw = e[2][...]
        gate = cw[0:1] * e[0][...] + cw[1:2] * e[1][...] + cw[2:3] * g + e[3][...]
        o[0][...] = (jax.nn.silu(gate) * u).astype(bf16)
        o[1][...] = g

    h, g = fused_mm([n2], [(w_gate, 0, D_MODEL, 0, 0), (w_up, 0, D_MODEL, 0, 0)],
                    [(state2d, (M, tn), lambda n, m: (0, n)), (state2d, (M, tn), lambda n, m: (0, n + F // tn)),
                     (cwp, (8, tn), lambda n, m: (0, n)), (conv_b.reshape(1, F), (1, tn), lambda n, m: (0, n))],
                    [(jax.ShapeDtypeStruct((M, F), bf16), (M, tn), lambda n, m: (m, n)),
                     (jax.ShapeDtypeStruct((M, F), f32), (M, tn), lambda n, m: (m, n))],
                    epi, M=M, tm=M, tn=tn, n_tiles=F // tn, vmem_mb=56)
    return h, g


def ffn_down(x1, h, w_down, tm):
    M = x1.shape[0]
    tn = 256
    F = w_down.shape[0]

    def epi(accs, e, o, s):
        o[0][...] = e[0][...] + accs[0]

    (x2,) = fused_mm([h], [(w_down, 0, F, 0, 0)], [(x1, (tm, tn), lambda n, m: (m, n))],
                     [(jax.ShapeDtypeStruct((M, D_MODEL), f32), (tm, tn), lambda n, m: (m, n))],
                     epi, M=M, tm=tm, tn=tn, n_tiles=D_MODEL // tn, vmem_mb=56)
    return x2


def ple_mix(x2, p, w_ple_gate, w_ple, tm):
    M = x2.shape[0]
    tn = 512

    def epi(accs, e, o, s):
        o[0][...] = e[0][...] + jax.nn.sigmoid(accs[0]) * accs[1]

    (x3,) = fused_mm([x2, p], [(w_ple_gate, 0, D_MODEL, 0, 0), (w_ple, 1, PLE_DIM, 0, 0)],
                     [(x2, (tm, tn), lambda n, m: (m, n))],
                     [(jax.ShapeDtypeStruct((M, D_MODEL), f32), (tm, tn), lambda n, m: (m, n))],
                     epi, M=M, tm=tm, tn=tn, n_tiles=D_MODEL // tn)
    return x3


def _phi_weights(phi_pos, phi_w1, phi_w2):
    half = CMP_BLOCK // 2
    w1a = phi_w1[:, :half].reshape(2, half * HEAD_DIM, CMP_HIDDEN)
    w1b = phi_w1[:, half:].reshape(2, half * HEAD_DIM, CMP_HIDDEN)
    w1cat = jnp.concatenate([w1a, w1b], axis=2).astype(bf16)
    pos8 = jnp.broadcast_to(phi_pos.reshape(2, 1, CMP_BLOCK * HEAD_DIM), (2, 8, CMP_BLOCK * HEAD_DIM))
    return w1cat, pos8, phi_w2.astype(bf16)


def _tail_block(x, p_l, cv, o, lw, ffn_fn, tm):
    (w_out, norm_ffn_g, w_ffn_down, w_ple, w_ple_gate, norm_final_g) = lw
    x1 = out_proj(x, cv, o, w_out, tm)
    n2 = rmsnorm_call(x1, norm_ffn_g, bf16, tm)
    h, ffn_aux = ffn_fn(n2)
    x2 = ffn_down(x1, h, w_ffn_down, tm)
    x3 = ple_mix(x2, p_l, w_ple_gate, w_ple, tm)
    return rmsnorm_call(x3, norm_final_g, f32, tm), ffn_aux


def kernel(x_prompt, x_sample, cache_cmp_kv, cache_slc_kv, state_win_kv, state_conv, state_ffn, page_table, p_prompt, p_sample, norm_mix_g, w_in, conv_w, conv_b, conv_ln_g, conv_ln_b, phi_pos, phi_w1, phi_w2, w_out, norm_ffn_g, w_ffn_gate, w_ffn_up, ffn_conv_w, ffn_conv_b, w_ffn_down, w_ple, w_ple_gate, norm_final_g):
    assert x_prompt.shape[0] == 1 and x_sample.shape[1] == 1 and norm_mix_g.shape[0] == 1
    T = x_prompt.shape[1]
    B = x_sample.shape[0]
    n_pages = page_table.shape[1]
    past_len = n_pages * PAGE_SIZE
    wc = state_win_kv.shape[2]
    l = 0
    w_in_l = w_in[l]
    gate_cols = w_in_l.shape[1] - (2 * CONV_CH + Q_COLS + 3 * KV_BRANCH_COLS)
    w_gates = jnp.pad(w_in_l[:, w_in_l.shape[1] - gate_cols:], ((0, 0), (0, LANES - gate_cols)))
    w1cat, pos8, w2b = _phi_weights(phi_pos[l], phi_w1[l], phi_w2[l])
    tail_w = (w_out[l], norm_ffn_g[l], w_ffn_down[l], w_ple[l], w_ple_gate[l], norm_final_g)

    tm = 512
    xp = x_prompt[0]
    n = rmsnorm_call(xp, norm_mix_g[l], bf16, tm)
    cos_t, sin_t = rope_tables(jnp.arange(T))
    u, q, kv, kvb, gates = mixer_proj(n, w_in_l, w_gates, cos_t, sin_t, tm)
    cv = conv_prompt(u, jnp.zeros((CONV_WIDTH - 1, CONV_CH), f32), conv_w[l], conv_b[l], conv_ln_g[l], conv_ln_b[l])
    kvc = compress_prompt(kv[0], w1cat, pos8, w2b)
    o_c, selb = cmp_prompt(q, kvc)
    o = attn_prompt(q, selb, kvb, o_c, gates)
    ffn_p = functools.partial(ffn_hidden_prompt, buf=jnp.zeros((2, FFN_DIM), f32), w_gate=w_ffn_gate[l],
                              w_up=w_ffn_up[l], conv_w=ffn_conv_w[l], conv_b=ffn_conv_b[l], tm=tm)
    y_p, g_tail = _tail_block(xp, p_prompt[l, 0], cv, o, tail_w, ffn_p, tm)

    kv_shape = (1, 1, T, N_KV_HEADS, 2, HEAD_DIM)
    p_cmp = kv[0].reshape(kv_shape)
    p_slc = kv[1].reshape(kv_shape)
    full_w = jnp.concatenate([jnp.zeros((wc, KV_BRANCH_COLS), f32), kv[2]], axis=0)
    p_win = full_w[full_w.shape[0] - wc:].reshape(1, 1, wc, N_KV_HEADS, 2, HEAD_DIM)
    full_c = jnp.concatenate([jnp.zeros((CONV_WIDTH - 1, CONV_CH), f32), u], axis=0)
    p_conv = full_c[full_c.shape[0] - (CONV_WIDTH - 1):].reshape(1, 1, CONV_WIDTH - 1, CONV_CH)
    p_ffn = g_tail[6:8].reshape(1, 1, 2, FFN_DIM)

    xs = x_sample[:, 0]
    ns = rmsnorm_call(xs, norm_mix_g[l], bf16, B)
    cos_s, sin_s = rope_tables(jnp.full((B,), past_len, jnp.int32))
    us, qs, kvs, _, gates_s = mixer_proj(ns, w_in_l, w_gates, cos_s, sin_s, B)
    cvs = conv_sample(us, state_conv[l], conv_w[l], conv_b[l], conv_ln_g[l], conv_ln_b[l])
    cache_c2d = cache_cmp_kv[l].reshape(-1, PAGE_SIZE, KV_BRANCH_COLS)
    cache_s2d = cache_slc_kv[l].reshape(-1, PAGE_SIZE, KV_BRANCH_COLS)
    kvc_s = compress_sample(cache_c2d, page_table, w1cat, pos8, w2b)
    n_cmp_s = (past_len + 1 - CMP_BLOCK) // CMP_STRIDE + 1
    n_sel_s = -(-(past_len + 1) // SEL_BLOCK)
    assert n_cmp_s <= kvc_s.shape[2] and n_sel_s <= LANES
    q16 = jnp.pad(qs.reshape(B, N_KV_HEADS, GROUP, HEAD_DIM), ((0, 0), (0, 0), (0, QROWS - GROUP), (0, 0)))
    oc_s, selb_s = cmp_sample(q16, kvc_s, n_cmp_s, n_sel_s, past_len)
    kv_new = kvs.transpose(1, 0, 2)
    win2d = state_win_kv[l].reshape(B, wc, KV_BRANCH_COLS)
    o16, win_new = attn_sample(q16, selb_s.reshape(B, N_KV_HEADS, LANES), kv_new, oc_s,
                               gates_s.reshape(B, 1, LANES), win2d, cache_s2d, page_table)
    os_ = o16[:, :, :GROUP].reshape(B, Q_COLS).astype(bf16)
    ffn_s = functools.partial(ffn_hidden_sample, state2d=state_ffn[l].reshape(B, 2 * FFN_DIM), w_gate=w_ffn_gate[l],
                              w_up=w_ffn_up[l], conv_w=ffn_conv_w[l], conv_b=ffn_conv_b[l])
    y_s, g_s = _tail_block(xs, p_sample[l, :, 0], cvs, os_, tail_w, ffn_s, B)

    skv_shape = (1, B, 1, N_KV_HEADS, 2, HEAD_DIM)
    s_cmp = kvs[0].reshape(skv_shape)
    s_slc = kvs[1].reshape(skv_shape)
    s_win = win_new.reshape(1, B, wc, N_KV_HEADS, 2, HEAD_DIM)
    s_conv = jnp.concatenate([state_conv[l][:, 1:], us[:, None]], axis=1)[None]
    s_ffn = jnp.concatenate([state_ffn[l][:, 1:], g_s[:, None]], axis=1)[None]

    return (y_p[None], y_s[:, None], p_cmp, p_slc, p_win, p_conv, p_ffn, s_cmp, s_slc, s_win, s_conv, s_ffn)
```
